```python
import jax, jax.numpy as jnp
from jax import lax
import numpy as np

D_MODEL = 1024
BATCH = 8
SEQ = 2048
DEPTH = 1
DEC_BATCH = 128
DEC_SEQ = 1
PAST_LEN = 8192
PAGE_SIZE = 128

RW_HEADS = 8
RW_HEAD_DIM = 64
RW_WIDTH = RW_HEADS * RW_HEAD_DIM
DECAY_LORA = 64
ICLR_LORA = 64
LNX_EPS = 64e-5
SB_HEADS = 8
SB_HEAD_DIM = 64
SB_WIDTH = SB_HEADS * SB_HEAD_DIM
SB_SCALE = 1.0 / float(np.sqrt(SB_HEAD_DIM))
SB_BIAS_INIT = -6.0
Q_BLOCK = 128
N_MEM = 256
MEM_HEADS = 4
MEM_HEAD_DIM = 128
MEM_WIDTH = MEM_HEADS * MEM_HEAD_DIM
MEM_SCALE = 1.0 / float(np.sqrt(MEM_HEAD_DIM))
N_BRANCH = 3
PEER_HEADS = 8
PEER_N_KEYS = 128
PEER_KEY_DIM = 256
PEER_TOPK = 16
N_EXPERTS = PEER_N_KEYS * PEER_N_KEYS
PEER_BLOCK = 128
NORM_EPS = 1e-6
RW_COLS = 3 * RW_WIDTH + DECAY_LORA + ICLR_LORA
SB_COLS = 3 * SB_WIDTH
MEM_COLS = MEM_WIDTH
GATE_COLS = N_BRANCH * D_MODEL
IN_COLS = RW_COLS + SB_COLS + MEM_COLS + GATE_COLS

kernel_name = "rwkv7_stickbreak_memory_peer_hybrid_step"


def rms_norm(x, g, eps=NORM_EPS):
    xf = x.astype(jnp.float32)
    y = xf * lax.rsqrt(jnp.mean(xf * xf, axis=-1, keepdims=True) + eps)
    return (y * g.astype(jnp.float32)).astype(x.dtype)


def rwkv7_scan(s0, r, decay, k, v, kk, a):
    def step(s, inp):
        r_t, w_t, k_t, v_t, kk_t, a_t = inp
        sa = jnp.einsum('bhij,bhj->bhi', s, -kk_t)
        s = s * w_t[:, :, None, :] + sa[..., None] * (kk_t * a_t)[:, :, None, :] + v_t[..., None] * k_t[:, :, None, :]
        return s, jnp.einsum('bhij,bhj->bhi', s, r_t)
    xs = tuple(jnp.swapaxes(t.astype(jnp.float32), 0, 1) for t in (r, decay, k, v, kk, a))
    s_fin, ys = lax.scan(step, s0.astype(jnp.float32), xs)
    return jnp.swapaxes(ys, 0, 1), s_fin


def rwkv7_branch(p_cur, p_prev, s0, lw):
    B, T, _ = p_cur.shape
    f32 = jnp.float32
    feat = (p_cur + lw['rwkv_mu'] * (p_prev - p_cur)).astype(f32)
    r, k, v, wl, al = jnp.split(feat, [RW_WIDTH, 2 * RW_WIDTH, 3 * RW_WIDTH, 3 * RW_WIDTH + DECAY_LORA], axis=-1)
    w = -jax.nn.softplus(-(lw['rwkv_w0'].astype(f32) + jnp.tanh(wl) @ lw['rwkv_w2'].astype(f32))) - 0.5
    decay = jnp.exp(-jnp.exp(w))
    a = jax.nn.sigmoid(lw['rwkv_a0'].astype(f32) + al @ lw['rwkv_a2'].astype(f32))
    heads = lambda t: t.reshape(B, T, RW_HEADS, RW_HEAD_DIM)
    kk = heads(k * lw['rwkv_k_k'].astype(f32))
    kk = kk / jnp.maximum(jnp.sqrt(jnp.sum(kk * kk, axis=-1, keepdims=True)), 1e-12)
    k = k * (1.0 + (a - 1.0) * lw['rwkv_k_a'].astype(f32))
    r, k, v, a, decay = heads(r), heads(k), heads(v), heads(a), heads(decay)
    y, s_new = rwkv7_scan(s0, r, decay, k, v, kk, a)
    mu = jnp.mean(y, axis=-1, keepdims=True)
    var = jnp.mean(jnp.square(y - mu), axis=-1, keepdims=True)
    y = ((y - mu) * lax.rsqrt(var + LNX_EPS)).reshape(B, T, RW_WIDTH)
    y = y * lw['rwkv_lnx_g'].astype(f32) + lw['rwkv_lnx_b'].astype(f32)
    bonus = jnp.sum(r * k * lw['rwkv_r_k'].astype(f32), axis=-1, keepdims=True) * v
    y = y + bonus.reshape(B, T, RW_WIDTH)
    return y.astype(p_cur.dtype), s_new


def sb_block(q, ks, vs, q_pos, k_pos, bias):
    z = jnp.concatenate([jnp.einsum('bqhd,bkhd->bhqk', q, k) for k in ks], axis=-1).astype(jnp.float32) * SB_SCALE
    z = z + bias.astype(jnp.float32)[None, :, None, None]
    causal = k_pos[None, :] < q_pos[:, None]
    log_fail = jnp.where(causal, -jax.nn.softplus(z), 0.0)
    log_after = lax.cumsum(log_fail, axis=3, reverse=True) - log_fail
    att = jnp.where(causal, jnp.exp(jax.nn.log_sigmoid(z) + log_after), 0.0)
    out = None
    off = 0
    for v in vs:
        n = v.shape[1]
        part = jnp.einsum('bhqk,bkhd->bqhd', att[..., off:off + n].astype(v.dtype), v)
        out = part if out is None else out + part
        off += n
    return out


def stick_breaking(q, ks, vs, q_pos, k_pos, bias):
    B, Q, H, Dh = q.shape
    if Q <= Q_BLOCK or Q % Q_BLOCK:
        return sb_block(q, ks, vs, q_pos, k_pos, bias)
    nb = Q // Q_BLOCK
    qb = jnp.moveaxis(q.reshape(B, nb, Q_BLOCK, H, Dh), 1, 0)
    pb = q_pos.reshape(nb, Q_BLOCK)
    ob = lax.map(lambda args: sb_block(args[0], ks, vs, args[1], k_pos, bias), (qb, pb))
    return jnp.moveaxis(ob, 0, 1).reshape(B, Q, H, Dh)


def memory_kv(mem, mem_norm_g, w_mem_kv, k_norm_g):
    B, M, _ = mem.shape
    kv = rms_norm(mem, mem_norm_g) @ w_mem_kv
    mk, mv = jnp.split(kv, 2, axis=-1)
    mk = rms_norm(mk.reshape(B, M, MEM_HEADS, MEM_HEAD_DIM), k_norm_g)
    return mk, mv.reshape(B, M, MEM_HEADS, MEM_HEAD_DIM)


def memory_attention(q, mk, mv):
    s = jnp.einsum('bthd,bmhd->bhtm', q, mk).astype(jnp.float32) * MEM_SCALE
    p = jax.nn.softmax(s, axis=-1)
    return jnp.einsum('bhtm,bmhd->bthd', p.astype(mv.dtype), mv)


def peer_route(xn, w_q, sub_keys):
    n = xn.shape[0]
    q = (xn @ w_q).reshape(n, PEER_HEADS, 2, PEER_KEY_DIM // 2)
    s = jnp.einsum('nhpc,hpkc->nhpk', q, sub_keys).astype(jnp.float32)
    top_s, top_i = lax.top_k(s, PEER_TOPK)
    cand_s = (top_s[:, :, 0, :, None] + top_s[:, :, 1, None, :]).reshape(n, PEER_HEADS, PEER_TOPK * PEER_TOPK)
    cand_i = (top_i[:, :, 0, :, None] * PEER_N_KEYS + top_i[:, :, 1, None, :]).reshape(n, PEER_HEADS, PEER_TOPK * PEER_TOPK)
    best_s, pos = lax.top_k(cand_s, PEER_TOPK)
    idx = jnp.take_along_axis(cand_i, pos, axis=-1)
    return idx, jax.nn.softmax(best_s, axis=-1)


def peer_experts(xn, idx, gate, u_tab, v_tab):
    n, d = xn.shape
    pad = (-n) % PEER_BLOCK
    nb = (n + pad) // PEER_BLOCK
    xp = jnp.pad(xn, ((0, pad), (0, 0))).reshape(nb, PEER_BLOCK, d)
    ip = jnp.pad(idx, ((0, pad), (0, 0), (0, 0))).reshape(nb, PEER_BLOCK, PEER_HEADS, PEER_TOPK)
    gp = jnp.pad(gate, ((0, pad), (0, 0), (0, 0))).reshape(nb, PEER_BLOCK, PEER_HEADS, PEER_TOPK)

    def block(args):
        xb, ib, gb = args
        act = jax.nn.gelu(jnp.einsum('thkd,td->thk', u_tab[ib], xb).astype(jnp.float32), approximate=False)
        return jnp.einsum('thk,thkd->td', (gb * act).astype(v_tab.dtype), v_tab[ib])

    out = lax.map(block, (xp, ip, gp))
    return out.reshape(nb * PEER_BLOCK, d)[:n]


def hybrid_layer(x, shift_prev, rwkv_s0, sb_k_past, sb_v_past, mem_k, mem_v, lw):
    B, T, _ = x.shape
    past = 0 if sb_k_past is None else sb_k_past.shape[1]
    xn = rms_norm(x, lw['norm1_g'])
    proj = xn @ lw['w_in']
    p_rw, p_sb, p_mq, p_gate = jnp.split(proj, [RW_COLS, RW_COLS + SB_COLS, RW_COLS + SB_COLS + MEM_COLS], axis=-1)

    if shift_prev is None:
        prev0 = jnp.zeros((B, 1, RW_COLS), p_rw.dtype)
        rwkv_s0 = jnp.zeros((B, RW_HEADS, RW_HEAD_DIM, RW_HEAD_DIM), jnp.float32)
    else:
        prev0 = (shift_prev @ lw['w_in'][:, :RW_COLS])[:, None, :]
    p_rw_prev = jnp.concatenate([prev0, p_rw[:, :-1]], axis=1)
    y_rw, s_new = rwkv7_branch(p_rw, p_rw_prev, rwkv_s0, lw)

    q_sb, k_sb, v_sb = [t.reshape(B, T, SB_HEADS, SB_HEAD_DIM) for t in jnp.split(p_sb, 3, axis=-1)]
    q_sb = rms_norm(q_sb, lw['sb_q_norm_g'])
    k_sb = rms_norm(k_sb, lw['sb_k_norm_g'])
    q_pos = past + jnp.arange(T)
    if sb_k_past is None:
        ks, vs, k_pos = [k_sb], [v_sb], q_pos
    else:
        ks, vs, k_pos = [sb_k_past, k_sb], [sb_v_past, v_sb], jnp.arange(past + T)
    y_sb = stick_breaking(q_sb, ks, vs, q_pos, k_pos, lw['sb_logit_bias']).reshape(B, T, SB_WIDTH)

    q_m = rms_norm(p_mq.reshape(B, T, MEM_HEADS, MEM_HEAD_DIM), lw['mem_q_norm_g'])
    y_m = memory_attention(q_m, mem_k, mem_v).reshape(B, T, MEM_WIDTH)

    g_rw, g_sb, g_m = jnp.split(jax.nn.sigmoid(p_gate), N_BRANCH, axis=-1)
    h = g_rw * (y_rw @ lw['w_br_rwkv']) + g_sb * (y_sb @ lw['w_br_sb']) + g_m * (y_m @ lw['w_br_mem'])
    x = x + h @ lw['w_out']

    xn2 = rms_norm(x, lw['norm2_g']).reshape(B * T, D_MODEL)
    idx, gate = peer_route(xn2, lw['peer_w_q'], lw['peer_sub_keys'])
    x = x + peer_experts(xn2, idx, gate, lw['peer_u'], lw['peer_v']).reshape(B, T, D_MODEL)
    return x, k_sb, v_sb, s_new, xn[:, -1]


def setup_inputs(seed: int = 0) -> dict:
    key = jax.random.key(seed)
    keys = iter(jax.random.split(key, 64))
    f32 = jnp.float32

    def nrm(shape, scale=1.0):
        return scale * jax.random.normal(next(keys), shape, f32)

    def gain(shape):
        return 1.0 + 0.02 * jax.random.normal(next(keys), shape, f32)

    L = DEPTH
    n_pages = PAST_LEN // PAGE_SIZE
    n_used = DEC_BATCH * n_pages
    n_pool = n_used + (n_used + 3) // 4
    page_table = jax.random.permutation(next(keys), n_pool)[:n_used].reshape(DEC_BATCH, n_pages).astype(jnp.int32)
    return {
        'x_prompt': nrm((BATCH, SEQ, D_MODEL)),
        'x_sample': nrm((DEC_BATCH, DEC_SEQ, D_MODEL)),
        'cache_sb_k': nrm((L, n_pool, PAGE_SIZE, SB_HEADS, SB_HEAD_DIM)),
        'cache_sb_v': nrm((L, n_pool, PAGE_SIZE, SB_HEADS, SB_HEAD_DIM)),
        'cache_mem_k': nrm((L, DEC_BATCH, N_MEM, MEM_HEADS, MEM_HEAD_DIM)),
        'cache_mem_v': nrm((L, DEC_BATCH, N_MEM, MEM_HEADS, MEM_HEAD_DIM)),
        'state_rwkv': nrm((L, DEC_BATCH, RW_HEADS, RW_HEAD_DIM, RW_HEAD_DIM), 0.3),
        'state_shift': nrm((L, DEC_BATCH, D_MODEL)),
        'page_table': page_table,
        'mem_prompt': nrm((BATCH, N_MEM, D_MODEL)),
        'norm1_g': gain((L, D_MODEL)),
        'w_in': nrm((L, D_MODEL, IN_COLS), D_MODEL ** -0.5),
        'rwkv_mu': jax.random.uniform(next(keys), (L, RW_COLS), f32),
        'rwkv_w0': -1.0 + 0.5 * jax.random.normal(next(keys), (L, RW_WIDTH), f32),
        'rwkv_w2': nrm((L, DECAY_LORA, RW_WIDTH), 0.1),
        'rwkv_a0': nrm((L, RW_WIDTH), 0.1),
        'rwkv_a2': nrm((L, ICLR_LORA, RW_WIDTH), 0.1),
        'rwkv_k_k': 0.85 + 0.05 * jax.random.normal(next(keys), (L, RW_WIDTH), f32),
        'rwkv_k_a': gain((L, RW_WIDTH)),
        'rwkv_r_k': nrm((L, RW_HEADS, RW_HEAD_DIM), 0.1),
        'rwkv_lnx_g': gain((L, RW_WIDTH)),
        'rwkv_lnx_b': nrm((L, RW_WIDTH), 0.02),
        'sb_q_norm_g': gain((L, SB_HEAD_DIM)),
        'sb_k_norm_g': gain((L, SB_HEAD_DIM)),
        'sb_logit_bias': SB_BIAS_INIT + 0.3 * jax.random.normal(next(keys), (L, SB_HEADS), f32),
        'mem_norm_g': gain((L, D_MODEL)),
        'w_mem_kv': nrm((L, D_MODEL, 2 * MEM_WIDTH), D_MODEL ** -0.5),
        'mem_q_norm_g': gain((L, MEM_HEAD_DIM)),
        'mem_k_norm_g': gain((L, MEM_HEAD_DIM)),
        'w_br_rwkv': nrm((L, RW_WIDTH, D_MODEL), RW_WIDTH ** -0.5),
        'w_br_sb': nrm((L, SB_WIDTH, D_MODEL), SB_WIDTH ** -0.5),
        'w_br_mem': nrm((L, MEM_WIDTH, D_MODEL), MEM_WIDTH ** -0.5),
        'w_out': nrm((L, D_MODEL, D_MODEL), D_MODEL ** -0.5),
        'norm2_g': gain((L, D_MODEL)),
        'peer_w_q': nrm((L, D_MODEL, PEER_HEADS * PEER_KEY_DIM), D_MODEL ** -0.5),
        'peer_sub_keys': nrm((L, PEER_HEADS, 2, PEER_N_KEYS, PEER_KEY_DIM // 2), (PEER_KEY_DIM // 2) ** -0.5),
        'peer_u': nrm((L, N_EXPERTS, D_MODEL), D_MODEL ** -0.5),
        'peer_v': nrm((L, N_EXPERTS, D_MODEL), PEER_HEADS ** -0.5),
    }


def reference(x_prompt, x_sample, cache_sb_k, cache_sb_v, cache_mem_k, cache_mem_v,
              state_rwkv, state_shift, page_table, mem_prompt,
              norm1_g, w_in, rwkv_mu, rwkv_w0, rwkv_w2, rwkv_a0, rwkv_a2, rwkv_k_k, rwkv_k_a,
              rwkv_r_k, rwkv_lnx_g, rwkv_lnx_b, sb_q_norm_g, sb_k_norm_g, sb_logit_bias,
              mem_norm_g, w_mem_kv, mem_q_norm_g, mem_k_norm_g,
              w_br_rwkv, w_br_sb, w_br_mem, w_out,
              norm2_g, peer_w_q, peer_sub_keys, peer_u, peer_v):
    dec_batch = x_sample.shape[0]
    past_len = page_table.shape[1] * cache_sb_k.shape[2]
    y_p, y_s = x_prompt, x_sample
    sbk_p, sbv_p, memk_p, memv_p, st_p, sh_p = [], [], [], [], [], []
    sbk_s, sbv_s, st_s, sh_s = [], [], [], []
    for l in range(DEPTH):
        lw = dict(norm1_g=norm1_g[l], w_in=w_in[l], rwkv_mu=rwkv_mu[l], rwkv_w0=rwkv_w0[l],
                  rwkv_w2=rwkv_w2[l], rwkv_a0=rwkv_a0[l], rwkv_a2=rwkv_a2[l], rwkv_k_k=rwkv_k_k[l],
                  rwkv_k_a=rwkv_k_a[l], rwkv_r_k=rwkv_r_k[l], rwkv_lnx_g=rwkv_lnx_g[l],
                  rwkv_lnx_b=rwkv_lnx_b[l], sb_q_norm_g=sb_q_norm_g[l], sb_k_norm_g=sb_k_norm_g[l],
                  sb_logit_bias=sb_logit_bias[l],
                  mem_q_norm_g=mem_q_norm_g[l], w_br_rwkv=w_br_rwkv[l], w_br_sb=w_br_sb[l],
                  w_br_mem=w_br_mem[l], w_out=w_out[l], norm2_g=norm2_g[l], peer_w_q=peer_w_q[l],
                  peer_sub_keys=peer_sub_keys[l], peer_u=peer_u[l], peer_v=peer_v[l])
        mk_p, mv_p = memory_kv(mem_prompt, mem_norm_g[l], w_mem_kv[l], mem_k_norm_g[l])
        y_p, k_new, v_new, s_new, shift_new = hybrid_layer(y_p, None, None, None, None, mk_p, mv_p, lw)
        sbk_p.append(k_new); sbv_p.append(v_new); memk_p.append(mk_p); memv_p.append(mv_p)
        st_p.append(s_new); sh_p.append(shift_new)
        k_past = cache_sb_k[l][page_table].reshape(dec_batch, past_len, SB_HEADS, SB_HEAD_DIM)
        v_past = cache_sb_v[l][page_table].reshape(dec_batch, past_len, SB_HEADS, SB_HEAD_DIM)
        y_s, k_new, v_new, s_new, shift_new = hybrid_layer(
            y_s, state_shift[l], state_rwkv[l], k_past, v_past, cache_mem_k[l], cache_mem_v[l], lw)
        sbk_s.append(k_new); sbv_s.append(v_new); st_s.append(s_new); sh_s.append(shift_new)
    sb_k_prompt = jnp.stack(sbk_p); sb_v_prompt = jnp.stack(sbv_p)
    mem_k_prompt = jnp.stack(memk_p); mem_v_prompt = jnp.stack(memv_p)
    rwkv_state_prompt = jnp.stack(st_p); shift_prompt = jnp.stack(sh_p)
    sb_k_sample = jnp.stack(sbk_s); sb_v_sample = jnp.stack(sbv_s)
    rwkv_state_sample = jnp.stack(st_s); shift_sample = jnp.stack(sh_s)
    return (y_p, y_s, sb_k_prompt, sb_v_prompt, mem_k_prompt, mem_v_prompt, rwkv_state_prompt,
            shift_prompt, sb_k_sample, sb_v_sample, rwkv_state_sample, shift_sample)
```

```python
import functools

import numpy as np
import jax
import jax.numpy as jnp
from jax import lax
from jax.experimental import pallas as pl
from jax.experimental.pallas import tpu as pltpu

F32 = jnp.float32
BF16 = jnp.bfloat16

D_MODEL = 1024
HEAD64 = 64
N_HEAD64 = 8
WIDTH = 512
LORA = 64
RW_COLS = 3 * WIDTH + 2 * LORA
SB_COLS = 3 * WIDTH
SB_SCALE = 0.125
MEM_HEADS = 4
MEM_HEAD_DIM = 128
N_MEM = 256
MEM_SCALE = 1.0 / float(np.sqrt(MEM_HEAD_DIM))
PAGE = 128
PEER_HEADS = 8
PEER_KEYS = 128
PEER_TOPK = 16
N_EXPERTS = PEER_KEYS * PEER_KEYS
NORM_EPS = 1e-6
LNX_EPS = 64e-5
LANES = 128
VMEM_LIMIT = 56 * 1024 * 1024

_CANDS = [(k1, k2) for k1 in range(PEER_TOPK) for k2 in range(PEER_TOPK) if (k1 + 1) * (k2 + 1) <= PEER_TOPK]
_CAND_ROWS = 56


def _cparams(*sem):
    return pltpu.CompilerParams(dimension_semantics=sem, vmem_limit_bytes=VMEM_LIMIT)


def _split_dot(x, m):
    hi = x.astype(BF16)
    lo = (x - hi.astype(F32)).astype(BF16)
    return jnp.dot(hi, m, preferred_element_type=F32) + jnp.dot(lo, m, preferred_element_type=F32)


def _softplus(z):
    return jnp.maximum(z, 0.0) + jnp.log1p(jnp.exp(-jnp.abs(z)))


def _sigmoid(z):
    return 1.0 / (1.0 + jnp.exp(-z))


def _seg_matrix(width, seg, value):
    idx = np.arange(width) // seg
    return jnp.asarray(np.where(idx[:, None] == idx[None, :], value, 0.0), BF16)


def _const_spec(shape):
    return pl.BlockSpec(shape, lambda *_: (0,) * len(shape))


def _proj_kernel(x_ref, g_ref, wrw_ref, wsb_ref, wmq_ref, wg_ref,
                 orw_ref, osb_ref, omq_ref, og_ref, oxn_ref, *, norm):
    x = x_ref[...]
    if norm:
        xn = x * lax.rsqrt(jnp.mean(x * x, axis=-1, keepdims=True) + NORM_EPS) * g_ref[...]
    else:
        xn = x
    oxn_ref[...] = xn
    xb = xn.astype(BF16)
    for w_ref, o_ref in ((wrw_ref, orw_ref), (wsb_ref, osb_ref), (wmq_ref, omq_ref), (wg_ref, og_ref)):
        o_ref[...] = jnp.dot(xb, w_ref[...], preferred_element_type=F32)


def _proj(x, g, w_parts, norm):
    n = x.shape[0]
    tm = min(n, 256)
    widths = [w.shape[1] for w in w_parts]
    wspec = lambda w: pl.BlockSpec((D_MODEL, w), lambda i: (0, 0), pipeline_mode=pl.Buffered(1))
    return pl.pallas_call(
        functools.partial(_proj_kernel, norm=norm),
        grid=(n // tm,),
        in_specs=[pl.BlockSpec((tm, D_MODEL), lambda i: (i, 0)), _const_spec((1, D_MODEL))] + [wspec(w) for w in widths],
        out_specs=[pl.BlockSpec((tm, w), lambda i: (i, 0)) for w in widths] + [pl.BlockSpec((tm, D_MODEL), lambda i: (i, 0))],
        out_shape=[jax.ShapeDtypeStruct((n, w), F32) for w in widths] + [jax.ShapeDtypeStruct((n, D_MODEL), F32)],
        compiler_params=_cparams("parallel"),
        name="in_proj",
    )(x, g, *w_parts)


def _rw_prep_kernel(p_ref, prev_ref, mu_ref, wl_ref, w0_ref, a0_ref, kk_ref, ka_ref, rk_ref, seg_ref,
                    r_ref, w_ref, k_ref, v_ref, kkn_ref, b_ref, bonus_ref, carry_ref, *, sequential):
    p = p_ref[0]
    tt = p.shape[0]
    if sequential:
        first = pl.program_id(1) == 0
        prev_row = jnp.where(first, prev_ref[0], carry_ref[...])
        rolled = pltpu.roll(p, 1, axis=0)
        rowid = lax.broadcasted_iota(jnp.int32, p.shape, 0)
        p_prev = jnp.where(rowid == 0, prev_row, rolled)
        carry_ref[...] = p[tt - 1:tt, :]
    else:
        p_prev = prev_ref[0]
    feat = p + mu_ref[...] * (p_prev - p)
    r = feat[:, 0:WIDTH]
    k = feat[:, WIDTH:2 * WIDTH]
    v = feat[:, 2 * WIDTH:3 * WIDTH]
    lo = feat[:, 3 * WIDTH:]
    lane = lax.broadcasted_iota(jnp.int32, lo.shape, 1)
    lo = jnp.where(lane < LORA, jnp.tanh(lo), lo)
    lora = jnp.dot(lo.astype(BF16), wl_ref[...], preferred_element_type=F32)
    w = -_softplus(-(w0_ref[...] + lora[:, :WIDTH])) - 0.5
    decay = jnp.exp(-jnp.exp(w))
    a = _sigmoid(a0_ref[...] + lora[:, WIDTH:])
    seg = seg_ref[...]
    kk = k * kk_ref[...]
    kk = kk / jnp.maximum(jnp.sqrt(_split_dot(kk * kk, seg)), 1e-12)
    k2 = k * (1.0 + (a - 1.0) * ka_ref[...])
    r_ref[0] = r
    w_ref[0] = decay
    k_ref[0] = k2
    v_ref[0] = v
    kkn_ref[0] = kk
    b_ref[0] = -(kk * a)
    bonus_ref[0] = _split_dot(r * k2 * rk_ref[...], seg) * v


def _rw_prep(p_rw, prev, consts, sequential):
    b, t, _ = p_rw.shape
    tt = min(t, 256)
    blk = lambda w: pl.BlockSpec((1, tt, w), lambda i, j: (i, j, 0))
    prev_spec = pl.BlockSpec((1, 1, RW_COLS), lambda i, j: (i, 0, 0)) if sequential else blk(RW_COLS)
    cspecs = [_const_spec(c.shape) for c in consts]
    return pl.pallas_call(
        functools.partial(_rw_prep_kernel, sequential=sequential),
        grid=(b, t // tt),
        in_specs=[blk(RW_COLS), prev_spec] + cspecs,
        out_specs=[blk(WIDTH)] * 7,
        out_shape=[jax.ShapeDtypeStruct((b, t, WIDTH), F32)] * 7,
        scratch_shapes=[pltpu.VMEM((1, RW_COLS), F32)],
        compiler_params=_cparams("arbitrary", "arbitrary"),
        name="rwkv_prep",
    )(p_rw, prev, *consts)


def _rw_scan_kernel(r_ref, w_ref, k_ref, kk_ref, b_ref, v_ref, s0_ref, y_ref, s_ref, *, ni, tt):
    @pl.when(pl.program_id(1) == 0)
    def _():
        s_ref[...] = s0_ref[...]

    def step(t, carry):
        for i in range(ni):
            s = s_ref[0, i]
            sa = jnp.sum(s * kk_ref[0, t], axis=0, keepdims=True)
            vi = v_ref[0, t, pl.ds(i, 1), :]
            s = s * w_ref[0, t] + sa * b_ref[0, t] + vi * k_ref[0, t]
            s_ref[0, i] = s
            y_ref[0, t, pl.ds(i, 1), :] = jnp.sum(s * r_ref[0, t], axis=0, keepdims=True)
        return carry

    lax.fori_loop(0, tt, step, 0)


def _rw_scan(r, w, k, kk, bneg, v, s0):
    g, t, ni, _ = v.shape
    tt = min(t, 16)
    vec = pl.BlockSpec((1, tt, HEAD64, LANES), lambda i, j: (i, j, 0, 0))
    val = pl.BlockSpec((1, tt, ni, LANES), lambda i, j: (i, j, 0, 0))
    st = pl.BlockSpec((1, ni, HEAD64, LANES), lambda i, j: (i, 0, 0, 0))
    return pl.pallas_call(
        functools.partial(_rw_scan_kernel, ni=ni, tt=tt),
        grid=(g, t // tt),
        in_specs=[vec] * 5 + [val, st],
        out_specs=[val, st],
        out_shape=[jax.ShapeDtypeStruct((g, t, ni, LANES), F32), jax.ShapeDtypeStruct((g, ni, HEAD64, LANES), F32)],
        compiler_params=_cparams("parallel", "arbitrary"),
        name="rwkv_scan",
    )(r, w, k, kk, bneg, v, s0)


def _sb_prep_kernel(p_ref, gq_ref, gk_ref, seg_ref, qh_ref, q32_ref, kt_ref, vt_ref, ktb_ref, vtb_ref):
    p = p_ref[0]
    tt = p.shape[0]
    q, k, v = p[:, :WIDTH], p[:, WIDTH:2 * WIDTH], p[:, 2 * WIDTH:]
    seg = seg_ref[...]
    qn = q * lax.rsqrt(_split_dot(q * q, seg) + NORM_EPS) * gq_ref[...] * SB_SCALE
    kn = k * lax.rsqrt(_split_dot(k * k, seg) + NORM_EPS) * gk_ref[...]
    q32_ref[0] = qn
    for h in range(N_HEAD64):
        qh_ref[0, h] = qn[:, h * HEAD64:(h + 1) * HEAD64].astype(BF16)
    kt = kn.T.reshape(N_HEAD64, HEAD64, tt)
    vt = v.T.reshape(N_HEAD64, HEAD64, tt)
    kt_ref[0] = kt
    vt_ref[0] = vt
    ktb_ref[0, :, 0] = kt.astype(BF16)
    vtb_ref[0, :, 0] = vt.astype(BF16)


def _sb_prep(p_sb, gq, gk, seg):
    b, t, _ = p_sb.shape
    tt = PAGE
    nb = t // tt
    hd = pl.BlockSpec((1, N_HEAD64, HEAD64, tt), lambda i, j: (i, 0, 0, j))
    hdb = pl.BlockSpec((1, N_HEAD64, 1, HEAD64, tt), lambda i, j: (i, 0, j, 0, 0))
    return pl.pallas_call(
        _sb_prep_kernel,
        grid=(b, nb),
        in_specs=[pl.BlockSpec((1, tt, SB_COLS), lambda i, j: (i, j, 0)),
                  _const_spec((1, WIDTH)), _const_spec((1, WIDTH)), _const_spec((WIDTH, WIDTH))],
        out_specs=[pl.BlockSpec((1, N_HEAD64, tt, HEAD64), lambda i, j: (i, 0, j, 0)),
                   pl.BlockSpec((1, tt, WIDTH), lambda i, j: (i, j, 0)), hd, hd, hdb, hdb],
        out_shape=[jax.ShapeDtypeStruct((b, N_HEAD64, t, HEAD64), BF16),
                   jax.ShapeDtypeStruct((b, t, WIDTH), F32),
                   jax.ShapeDtypeStruct((b, N_HEAD64, HEAD64, t), F32),
                   jax.ShapeDtypeStruct((b, N_HEAD64, HEAD64, t), F32),
                   jax.ShapeDtypeStruct((b, N_HEAD64, nb, HEAD64, tt), BF16),
                   jax.ShapeDtypeStruct((b, N_HEAD64, nb, HEAD64, tt), BF16)],
        compiler_params=_cparams("parallel", "parallel"),
        name="sb_prep",
    )(p_sb, gq, gk, seg)


def _sb_attn_kernel(bias_ref, q_ref, kt_ref, vt_ref, tri_ref, o_ref):
    qi = pl.program_id(1)
    tri = tri_ref[...]
    row = lax.broadcasted_iota(jnp.int32, (PAGE, PAGE), 0)
    col = lax.broadcasted_iota(jnp.int32, (PAGE, PAGE), 1)
    causal = col < row
    outs = []
    for h in range(N_HEAD64):
        qh = q_ref[0, h]
        bias = bias_ref[h]

        def tile(kb, carry, acc, masked):
            z = jnp.dot(qh, kt_ref[0, h, kb], preferred_element_type=F32) + bias
            sp = _softplus(z)
            lf = -sp
            if masked:
                lf = jnp.where(causal, lf, 0.0)
            la = _split_dot(lf, tri) + carry
            att = jnp.exp(z - sp + la)
            if masked:
                att = jnp.where(causal, att, 0.0)
            acc = acc + lax.dot_general(att.astype(BF16), vt_ref[0, h, kb], (((1,), (1,)), ((), ())),
                                        preferred_element_type=F32)
            return carry + jnp.sum(lf, axis=1, keepdims=True), acc

        carry, acc = tile(qi, jnp.zeros((PAGE, 1), F32), jnp.zeros((PAGE, HEAD64), F32), True)
        carry, acc = lax.fori_loop(0, qi, lambda j, c: tile(qi - 1 - j, c[0], c[1], False), (carry, acc))
        outs.append(acc)
    o_ref[0] = jnp.concatenate(outs, axis=1)


def _sb_attn(bias, qh, ktb, vtb, tri):
    b, _, t, _ = qh.shape
    nb = t // PAGE
    kv = pl.BlockSpec((1, N_HEAD64, nb, HEAD64, PAGE), lambda i, j: (i, 0, 0, 0, 0))
    return pl.pallas_call(
        _sb_attn_kernel,
        grid=(b, nb),
        in_specs=[pl.BlockSpec(memory_space=pltpu.SMEM),
                  pl.BlockSpec((1, N_HEAD64, PAGE, HEAD64), lambda i, j: (i, 0, j, 0)), kv, kv,
                  _const_spec((PAGE, PAGE))],
        out_specs=pl.BlockSpec((1, PAGE, WIDTH), lambda i, j: (i, j, 0)),
        out_shape=jax.ShapeDtypeStruct((b, t, WIDTH), F32),
        compiler_params=_cparams("parallel", "arbitrary"),
        name="sb_attn",
    )(bias, qh, ktb, vtb, tri)


PAGES_PER_STEP = 8


def _sb_decode_kernel(pt_ref, qt_ref, bias_ref, tri_ref, *refs):
    del pt_ref
    k_refs = refs[:PAGES_PER_STEP]
    v_refs = refs[PAGES_PER_STEP:2 * PAGES_PER_STEP]
    o_ref, qb_ref, acc_ref, carry_ref = refs[2 * PAGES_PER_STEP:]
    g = pl.program_id(1)

    @pl.when(g == 0)
    def _():
        qt = qt_ref[0]
        for h in range(N_HEAD64):
            qb_ref[h] = jnp.broadcast_to(qt[:, h:h + 1], (HEAD64, LANES))
        acc_ref[...] = jnp.zeros_like(acc_ref)
        carry_ref[...] = jnp.zeros_like(carry_ref)

    tri = tri_ref[...]
    carry = carry_ref[...]
    for i in range(PAGES_PER_STEP):
        rows = [jnp.sum(k_refs[i][0, h] * qb_ref[h], axis=0, keepdims=True) for h in range(N_HEAD64)]
        z = jnp.concatenate(rows, axis=0) + bias_ref[...]
        sp = _softplus(z)
        lf = -sp
        la = _split_dot(lf, tri) + carry
        att = jnp.exp(z - sp + la)
        carry = carry + jnp.sum(lf, axis=1, keepdims=True)
        for h in range(N_HEAD64):
            acc_ref[h] += v_refs[i][0, h] * att[h:h + 1, :]
    carry_ref[...] = carry

    @pl.when(g == pl.num_programs(1) - 1)
    def _():
        o_ref[0] = jnp.sum(acc_ref[...].reshape(WIDTH, LANES), axis=1, keepdims=True)


def _sb_decode(page_table, qt, bias_col, tri, kt_cache, vt_cache):
    ns, n_pages = page_table.shape
    steps = n_pages // PAGES_PER_STEP

    def page_map(i, b, g, pt):
        return (pt[b * n_pages + (n_pages - 1 - (g * PAGES_PER_STEP + i))], 0, 0, 0)

    page_specs = [pl.BlockSpec((1, N_HEAD64, HEAD64, PAGE), functools.partial(page_map, i))
                  for i in range(PAGES_PER_STEP)]
    grid_spec = pltpu.PrefetchScalarGridSpec(
        num_scalar_prefetch=1,
        grid=(ns, steps),
        in_specs=[pl.BlockSpec((1, HEAD64, N_HEAD64), lambda b, g, pt: (b, 0, 0)),
                  pl.BlockSpec((N_HEAD64, 1), lambda b, g, pt: (0, 0)),
                  pl.BlockSpec((PAGE, PAGE), lambda b, g, pt: (0, 0))] + page_specs + page_specs,
        out_specs=pl.BlockSpec((1, WIDTH, 1), lambda b, g, pt: (b, 0, 0)),
        scratch_shapes=[pltpu.VMEM((N_HEAD64, HEAD64, LANES), F32),
                        pltpu.VMEM((N_HEAD64, HEAD64, LANES), F32),
                        pltpu.VMEM((N_HEAD64, 1), F32)],
    )
    out = pl.pallas_call(
        _sb_decode_kernel,
        grid_spec=grid_spec,
        out_shape=jax.ShapeDtypeStruct((ns, WIDTH, 1), F32),
        compiler_params=_cparams("parallel", "arbitrary"),
        name="sb_decode",
    )(page_table.reshape(-1), qt, bias_col, tri, *([kt_cache] * PAGES_PER_STEP), *([vt_cache] * PAGES_PER_STEP))
    return out.reshape(ns, WIDTH)


def _mem_kv_kernel(m_ref, g_ref, w_ref, gk_ref, k_ref, v_ref):
    x = m_ref[...]
    xn = x * lax.rsqrt(jnp.mean(x * x, axis=-1, keepdims=True) + NORM_EPS) * g_ref[...]
    kv = jnp.dot(xn.astype(BF16), w_ref[...], preferred_element_type=F32)
    for h in range(MEM_HEADS):
        kh = kv[:, h * MEM_HEAD_DIM:(h + 1) * MEM_HEAD_DIM]
        k_ref[:, h * MEM_HEAD_DIM:(h + 1) * MEM_HEAD_DIM] = (
            kh * lax.rsqrt(jnp.mean(kh * kh, axis=-1, keepdims=True) + NORM_EPS) * gk_ref[...])
    v_ref[...] = kv[:, WIDTH:]


def _mem_kv(mem, g, w, gk):
    n = mem.shape[0]
    tm = 256
    return pl.pallas_call(
        _mem_kv_kernel,
        grid=(n // tm,),
        in_specs=[pl.BlockSpec((tm, D_MODEL), lambda i: (i, 0)), _const_spec((1, D_MODEL)),
                  _const_spec((D_MODEL, 2 * WIDTH)), _const_spec((1, MEM_HEAD_DIM))],
        out_specs=[pl.BlockSpec((tm, WIDTH), lambda i: (i, 0))] * 2,
        out_shape=[jax.ShapeDtypeStruct((n, WIDTH), F32)] * 2,
        compiler_params=_cparams("parallel"),
        name="mem_kv",
    )(mem, g, w, gk)


def _mem_attn_kernel(q_ref, k_ref, v_ref, gq_ref, o_ref):
    q = q_ref[0]
    tq = q.shape[0]
    outs = []
    for h in range(MEM_HEADS):
        sl = slice(h * MEM_HEAD_DIM, (h + 1) * MEM_HEAD_DIM)
        qh = q[:, sl]
        qh = qh * lax.rsqrt(jnp.mean(qh * qh, axis=-1, keepdims=True) + NORM_EPS) * gq_ref[...]
        if tq < 8:
            qh = jnp.broadcast_to(qh, (8, MEM_HEAD_DIM))
        s = lax.dot_general(qh.astype(BF16), k_ref[0, :, sl].astype(BF16), (((1,), (1,)), ((), ())),
                            preferred_element_type=F32) * MEM_SCALE
        s = s - jnp.max(s, axis=-1, keepdims=True)
        e = jnp.exp(s)
        p = e / jnp.sum(e, axis=-1, keepdims=True)
        oh = jnp.dot(p.astype(BF16), v_ref[0, :, sl].astype(BF16), preferred_element_type=F32)
        outs.append(oh[:tq])
    o_ref[0] = jnp.concatenate(outs, axis=1)


def _mem_attn(q, mk, mv, gq):
    b, t, _ = q.shape
    tq = min(t, 512)
    kv = pl.BlockSpec((1, N_MEM, WIDTH), lambda i, j: (i, 0, 0))
    return pl.pallas_call(
        _mem_attn_kernel,
        grid=(b, t // tq),
        in_specs=[pl.BlockSpec((1, tq, WIDTH), lambda i, j: (i, j, 0)), kv, kv, _const_spec((1, MEM_HEAD_DIM))],
        out_specs=pl.BlockSpec((1, tq, WIDTH), lambda i, j: (i, j, 0)),
        out_shape=jax.ShapeDtypeStruct((b, t, WIDTH), F32),
        compiler_params=_cparams("parallel", "parallel"),
        name="mem_attn",
    )(q, mk, mv, gq)


def _merge_kernel(x_ref, y_ref, bonus_ref, ysb_ref, ym_ref, gate_ref, lg_ref, lb_ref, seg_ref,
                  wrw_ref, wsb_ref, wm_ref, wo_ref, g2_ref, x1_ref, xt_ref):
    seg = seg_ref[...]
    y = y_ref[...]
    mu = _split_dot(y, seg)
    yc = y - mu
    var = _split_dot(yc * yc, seg)
    y_rw = yc * lax.rsqrt(var + LNX_EPS) * lg_ref[...] + lb_ref[...] + bonus_ref[...]
    gate = _sigmoid(gate_ref[...])
    h = (gate[:, :D_MODEL] * jnp.dot(y_rw.astype(BF16), wrw_ref[...], preferred_element_type=F32)
         + gate[:, D_MODEL:2 * D_MODEL] * jnp.dot(ysb_ref[...].astype(BF16), wsb_ref[...], preferred_element_type=F32)
         + gate[:, 2 * D_MODEL:] * jnp.dot(ym_ref[...].astype(BF16), wm_ref[...], preferred_element_type=F32))
    x1 = x_ref[...] + jnp.dot(h.astype(BF16), wo_ref[...], preferred_element_type=F32)
    x1_ref[...] = x1
    xn2 = x1 * lax.rsqrt(jnp.mean(x1 * x1, axis=-1, keepdims=True) + NORM_EPS) * g2_ref[...]
    xt_ref[...] = xn2.T.astype(BF16)


def _merge(x, y, bonus, ysb, ym, gate, consts):
    n = x.shape[0]
    tm = min(n, 256)
    row = lambda w: pl.BlockSpec((tm, w), lambda i: (i, 0))
    return pl.pallas_call(
        _merge_kernel,
        grid=(n // tm,),
        in_specs=[row(D_MODEL), row(WIDTH), row(WIDTH), row(WIDTH), row(WIDTH), row(3 * D_MODEL)]
        + [_const_spec(c.shape) for c in consts],
        out_specs=[row(D_MODEL), pl.BlockSpec((D_MODEL, tm), lambda i: (0, i))],
        out_shape=[jax.ShapeDtypeStruct((n, D_MODEL), F32), jax.ShapeDtypeStruct((D_MODEL, n), BF16)],
        compiler_params=_cparams("parallel"),
        name="merge",
    )(x, y, bonus, ysb, ym, gate, *consts)


def _top16(s):
    iota = lax.broadcasted_iota(jnp.int32, s.shape, 0).astype(F32)
    rank = jnp.full(s.shape, float(PEER_TOPK), F32)
    vals = []
    for k in range(PEER_TOPK):
        m = jnp.max(s, axis=0, keepdims=True)
        idx = jnp.min(jnp.where(s == m, iota, float(PEER_KEYS)), axis=0, keepdims=True)
        sel = iota == idx
        rank = jnp.where(sel, float(k), rank)
        s = jnp.where(sel, -jnp.inf, s)
        vals.append(m)
    return vals, rank


def _peer_route_kernel(xt_ref, wq_ref, keys_ref, flat_ref, e0_ref, cnt_ref, e1_ref, rk1_ref):
    qt = jnp.dot(wq_ref[...], xt_ref[...], preferred_element_type=F32).astype(BF16)
    tn = qt.shape[1]
    flat = jnp.broadcast_to(flat_ref[...], (_CAND_ROWS, tn))
    for h in range(PEER_HEADS):
        s0 = jnp.dot(keys_ref[2 * h], qt[(2 * h) * PEER_KEYS:(2 * h + 1) * PEER_KEYS], preferred_element_type=F32)
        s1 = jnp.dot(keys_ref[2 * h + 1], qt[(2 * h + 1) * PEER_KEYS:(2 * h + 2) * PEER_KEYS],
                     preferred_element_type=F32)
        a, rank0 = _top16(s0)
        b, rank1 = _top16(s1)
        rows = [a[k1] + b[k2] for k1, k2 in _CANDS]
        rows += [jnp.full((1, tn), -jnp.inf, F32)] * (_CAND_ROWS - len(_CANDS))
        c = jnp.concatenate(rows, axis=0)
        beaten = jnp.zeros(c.shape, F32)
        for ri, (k1, k2) in enumerate(_CANDS):
            cr = rows[ri]
            wins = (cr > c) | ((cr == c) & (flat > float(k1 * PEER_TOPK + k2)))
            beaten = beaten + jnp.where(wins, 1.0, 0.0)
        sel = beaten < float(PEER_TOPK)
        ex = jnp.where(sel, jnp.exp(c - rows[0]), 0.0)
        z = jnp.sum(ex, axis=0, keepdims=True)
        self32 = jnp.where(sel, 1.0, 0.0)
        cnt = jnp.zeros(s0.shape, F32)
        ri = 0
        for k1 in range(PEER_TOPK):
            n_k2 = sum(1 for c1, _ in _CANDS if c1 == k1)
            c_k1 = jnp.sum(self32[ri:ri + n_k2], axis=0, keepdims=True)
            cnt = jnp.where(rank0 == float(k1), c_k1, cnt)
            ri += n_k2
        e0_ref[h] = jnp.exp(s0 - a[0]) / z
        e1_ref[h] = jnp.exp(s1 - b[0])
        cnt_ref[h] = cnt
        rk1_ref[h] = rank1


def _peer_route(xt, wq_t, keys, flat):
    n = xt.shape[1]
    tn = min(n, 256)
    out = pl.BlockSpec((PEER_HEADS, PEER_KEYS, tn), lambda i: (0, 0, i))
    return pl.pallas_call(
        _peer_route_kernel,
        grid=(n // tn,),
        in_specs=[pl.BlockSpec((D_MODEL, tn), lambda i: (0, i)), _const_spec(wq_t.shape),
                  _const_spec(keys.shape), _const_spec(flat.shape)],
        out_specs=[out] * 4,
        out_shape=[jax.ShapeDtypeStruct((PEER_HEADS, PEER_KEYS, n), F32)] * 4,
        compiler_params=_cparams("parallel"),
        name="peer_route",
    )(xt, wq_t, keys, flat)


EXPERT_TILE = 1024
_I1_PER_TILE = EXPERT_TILE // PEER_KEYS


def _peer_dense_kernel(xt_ref, u_ref, vt_ref, e0_ref, cnt_ref, e1_ref, rk1_ref, x1_ref, o_ref,
                       acc_ref, act_ref, g_ref):
    et = pl.program_id(1)

    @pl.when(et == 0)
    def _():
        acc_ref[...] = jnp.zeros_like(acc_ref)

    act_ref[...] = jnp.dot(u_ref[...], xt_ref[...], preferred_element_type=F32)
    for ii in range(_I1_PER_TILE):
        rows = slice(ii * PEER_KEYS, (ii + 1) * PEER_KEYS)
        w = None
        for h in range(PEER_HEADS):
            hit = rk1_ref[h] < cnt_ref[h, ii:ii + 1, :]
            t = jnp.where(hit, e1_ref[h], 0.0) * e0_ref[h, ii:ii + 1, :]
            w = t if w is None else w + t
        a = act_ref[rows, :]
        gelu = 0.5 * a * (1.0 + lax.erf(a * 0.7071067811865476))
        g_ref[rows, :] = (w * gelu).astype(BF16)
    acc_ref[...] += jnp.dot(vt_ref[...], g_ref[...], preferred_element_type=F32)

    @pl.when(et == pl.num_programs(1) - 1)
    def _():
        o_ref[...] = x1_ref[...] + acc_ref[...].T


def _peer_dense(xt, u, vt, e0, cnt, e1, rk1, x1):
    n = xt.shape[1]
    tn = min(n, 512)
    fac_row = pl.BlockSpec((PEER_HEADS, _I1_PER_TILE, tn), lambda i, j: (0, j, i))
    fac_all = pl.BlockSpec((PEER_HEADS, PEER_KEYS, tn), lambda i, j: (0, 0, i))
    return pl.pallas_call(
        _peer_dense_kernel,
        grid=(n // tn, N_EXPERTS // EXPERT_TILE),
        in_specs=[pl.BlockSpec((D_MODEL, tn), lambda i, j: (0, i)),
                  pl.BlockSpec((EXPERT_TILE, D_MODEL), lambda i, j: (j, 0)),
                  pl.BlockSpec((D_MODEL, EXPERT_TILE), lambda i, j: (0, j)),
                  fac_row, fac_row, fac_all, fac_all,
                  pl.BlockSpec((tn, D_MODEL), lambda i, j: (i, 0))],
        out_specs=pl.BlockSpec((tn, D_MODEL), lambda i, j: (i, 0)),
        out_shape=jax.ShapeDtypeStruct((n, D_MODEL), F32),
        scratch_shapes=[pltpu.VMEM((D_MODEL, tn), F32), pltpu.VMEM((EXPERT_TILE, tn), F32),
                        pltpu.VMEM((EXPERT_TILE, tn), BF16)],
        compiler_params=_cparams("parallel", "arbitrary"),
        name="peer_dense",
    )(xt, u, vt, e0, cnt, e1, rk1, x1)


def kernel(x_prompt, x_sample, cache_sb_k, cache_sb_v, cache_mem_k, cache_mem_v, state_rwkv, state_shift, page_table, mem_prompt, norm1_g, w_in, rwkv_mu, rwkv_w0, rwkv_w2, rwkv_a0, rwkv_a2, rwkv_k_k, rwkv_k_a, rwkv_r_k, rwkv_lnx_g, rwkv_lnx_b, sb_q_norm_g, sb_k_norm_g, sb_logit_bias, mem_norm_g, w_mem_kv, mem_q_norm_g, mem_k_norm_g, w_br_rwkv, w_br_sb, w_br_mem, w_out, norm2_g, peer_w_q, peer_sub_keys, peer_u, peer_v):
    assert w_in.shape[0] == 1, "single-layer step"
    bsz, seq, _ = x_prompt.shape
    ns = x_sample.shape[0]
    nh = N_HEAD64

    w_in_b = w_in[0].astype(BF16)
    w_parts = [w_in_b[:, :RW_COLS], w_in_b[:, RW_COLS:RW_COLS + SB_COLS],
               w_in_b[:, RW_COLS + SB_COLS:RW_COLS + SB_COLS + WIDTH], w_in_b[:, RW_COLS + SB_COLS + WIDTH:]]
    seg_mean = _seg_matrix(WIDTH, HEAD64, 1.0 / HEAD64)
    seg_sum = _seg_matrix(WIDTH, HEAD64, 1.0)
    idx = np.arange(PAGE)
    tri = jnp.asarray(np.where(idx[:, None] > idx[None, :], 1.0, 0.0), BF16)
    lora_w = jnp.zeros((2 * LORA, 2 * WIDTH), F32)
    lora_w = lora_w.at[:LORA, :WIDTH].set(rwkv_w2[0]).at[LORA:, WIDTH:].set(rwkv_a2[0]).astype(BF16)
    rw_consts = [rwkv_mu, lora_w, rwkv_w0, rwkv_a0, rwkv_k_k, rwkv_k_a, rwkv_r_k.reshape(1, WIDTH), seg_sum]
    gq_sb = jnp.tile(sb_q_norm_g, (1, nh))
    gk_sb = jnp.tile(sb_k_norm_g, (1, nh))
    merge_consts = [rwkv_lnx_g, rwkv_lnx_b, seg_mean, w_br_rwkv[0].astype(BF16), w_br_sb[0].astype(BF16),
                    w_br_mem[0].astype(BF16), w_out[0].astype(BF16), norm2_g]
    wq_t = peer_w_q[0].T.astype(BF16)
    keys = peer_sub_keys[0].reshape(2 * PEER_HEADS, PEER_KEYS, PEER_KEYS).astype(BF16)
    flat_np = np.full((_CAND_ROWS, 1), 1e9, np.float32)
    flat_np[:len(_CANDS), 0] = [k1 * PEER_TOPK + k2 for k1, k2 in _CANDS]
    flat = jnp.asarray(flat_np)
    u_b = peer_u[0].astype(BF16)
    vt_b = peer_v[0].T.astype(BF16)

    def peer(x1, xt):
        e0, cnt, e1, rk1 = _peer_route(xt, wq_t, keys, flat)
        return _peer_dense(xt, u_b, vt_b, e0, cnt, e1, rk1, x1)

    n_p = bsz * seq
    p_rw, p_sb, p_mq, p_gate, xn = _proj(x_prompt.reshape(n_p, D_MODEL), norm1_g, w_parts, True)
    shift_prompt = xn.reshape(bsz, seq, D_MODEL)[:, -1][None]

    r, w, k, v, kk, bneg, bonus = _rw_prep(p_rw.reshape(bsz, seq, RW_COLS), jnp.zeros((bsz, 1, RW_COLS), F32),
                                           rw_consts, True)

    def key_major(a):
        a = a.reshape(bsz, seq, nh, HEAD64).transpose(1, 3, 0, 2).reshape(seq, HEAD64, bsz * nh)
        return jnp.concatenate([a, a], axis=-1)[None]

    half = HEAD64 // 2
    v_s = v.reshape(bsz, seq, nh, 2, half).transpose(1, 4, 3, 0, 2).reshape(1, seq, half, 2 * bsz * nh)
    y_s, s_fin = _rw_scan(key_major(r), key_major(w), key_major(k), key_major(kk), key_major(bneg), v_s,
                          jnp.zeros((1, half, HEAD64, LANES), F32))
    y_rw = y_s.reshape(seq, half, 2, bsz, nh).transpose(3, 0, 4, 2, 1).reshape(n_p, WIDTH)
    rwkv_state_prompt = s_fin.reshape(half, HEAD64, 2, bsz, nh).transpose(3, 4, 2, 0, 1).reshape(
        1, bsz, nh, HEAD64, HEAD64)

    qh, _, kt, vt, ktb, vtb = _sb_prep(p_sb.reshape(bsz, seq, SB_COLS), gq_sb, gk_sb, seg_mean)
    y_sb = _sb_attn(sb_logit_bias[0], qh, ktb, vtb, tri)
    sb_k_prompt = kt.transpose(0, 3, 1, 2)[None]
    sb_v_prompt = vt.transpose(0, 3, 1, 2)[None]

    mk, mv = _mem_kv(mem_prompt.reshape(bsz * N_MEM, D_MODEL), mem_norm_g, w_mem_kv[0].astype(BF16), mem_k_norm_g)
    y_m = _mem_attn(p_mq.reshape(bsz, seq, WIDTH), mk.reshape(bsz, N_MEM, WIDTH), mv.reshape(bsz, N_MEM, WIDTH),
                    mem_q_norm_g)
    mem_k_prompt = mk.reshape(1, bsz, N_MEM, MEM_HEADS, MEM_HEAD_DIM)
    mem_v_prompt = mv.reshape(1, bsz, N_MEM, MEM_HEADS, MEM_HEAD_DIM)

    x1, xt = _merge(x_prompt.reshape(n_p, D_MODEL), y_rw, bonus.reshape(n_p, WIDTH), y_sb.reshape(n_p, WIDTH),
                    y_m.reshape(n_p, WIDTH), p_gate, merge_consts)
    y_prompt = peer(x1, xt).reshape(bsz, seq, D_MODEL)

    ps_rw, ps_sb, ps_mq, ps_gate, xn_s = _proj(x_sample.reshape(ns, D_MODEL), norm1_g, w_parts, True)
    shift_sample = xn_s[None]
    prev_s = _proj(state_shift[0], norm1_g, w_parts, False)[0]
    r, w, k, v, kk, bneg, bonus_s = _rw_prep(ps_rw[None], prev_s[None], rw_consts, False)

    def head_major(a):
        return a.reshape(ns, nh, HEAD64).transpose(1, 2, 0)[:, None]

    y_s, s_new = _rw_scan(head_major(r), head_major(w), head_major(k), head_major(kk), head_major(bneg),
                          head_major(v), state_rwkv[0].transpose(1, 2, 3, 0))
    ys_rw = y_s[:, 0].transpose(2, 0, 1).reshape(ns, WIDTH)
    rwkv_state_sample = s_new.transpose(3, 0, 1, 2)[None]

    _, q32, kt_s, vt_s, _, _ = _sb_prep(ps_sb[None], gq_sb, gk_sb, seg_mean)
    sb_k_sample = kt_s[0].transpose(2, 0, 1).reshape(1, ns, 1, nh, HEAD64)
    sb_v_sample = vt_s[0].transpose(2, 0, 1).reshape(1, ns, 1, nh, HEAD64)
    qt = q32[0].reshape(ns, nh, HEAD64).transpose(0, 2, 1)
    ys_sb = _sb_decode(page_table, qt, sb_logit_bias.reshape(nh, 1), tri,
                       cache_sb_k[0].transpose(0, 2, 3, 1), cache_sb_v[0].transpose(0, 2, 3, 1))

    ys_m = _mem_attn(ps_mq.reshape(ns, 1, WIDTH), cache_mem_k[0].reshape(ns, N_MEM, WIDTH),
                     cache_mem_v[0].reshape(ns, N_MEM, WIDTH), mem_q_norm_g).reshape(ns, WIDTH)

    x1_s, xt_s = _merge(x_sample.reshape(ns, D_MODEL), ys_rw, bonus_s[0], ys_sb, ys_m, ps_gate, merge_consts)
    y_sample = peer(x1_s, xt_s).reshape(ns, 1, D_MODEL)

    return (y_prompt, y_sample, sb_k_prompt, sb_v_prompt, mem_k_prompt, mem_v_prompt, rwkv_state_prompt,
            shift_prompt, sb_k_sample, sb_v_sample, rwkv_state_sample, shift_sample)
```
